```python
import jax, jax.numpy as jnp
from jax import lax
import numpy as np

D_MODEL = 1024
BATCH = 32
SEQ = 2048
DEPTH = 4

GRID_W = 64
CTX_LEN = 256
MIX_WIDTH = D_MODEL
CONV_A_WIDTH = D_MODEL // 4
CONV_A_K = 3
CONV_B_WIDTH = D_MODEL // 4
CONV_B_K = 31
MLA_HEADS = 8
QK_NOPE = 64
QK_ROPE = 32
V_DIM = 64
Q_LORA = 256
KV_LORA = 128
MLA_WIDTH = MLA_HEADS * V_DIM
PROJ_WIDTH = 3 * CONV_A_WIDTH + 2 * CONV_B_WIDTH + Q_LORA + KV_LORA + QK_ROPE
ATTN_SCALE = (QK_NOPE + QK_ROPE) ** -0.5
Q_BLOCK = 128
ROPE_BASE = 10000.0
PEER_HEADS = 8
N_KEYS = 128
N_EXPERTS = N_KEYS * N_KEYS
PK_HALF = 128
PEER_TOPK = 16
PEER_CHUNK = 128
EPS = 1e-6

kernel_name = "hybrid_conv_mla_peer_diffusion_trunk"


def rmsnorm(x, g):
    xf = x.astype(jnp.float32)
    y = xf * lax.rsqrt(jnp.mean(xf * xf, axis=-1, keepdims=True) + EPS)
    return (y * g.astype(jnp.float32)).astype(x.dtype)


def layernorm(x, g, b):
    xf = x.astype(jnp.float32)
    mu = jnp.mean(xf, axis=-1, keepdims=True)
    var = jnp.mean(jnp.square(xf - mu), axis=-1, keepdims=True)
    y = (xf - mu) * lax.rsqrt(var + EPS)
    return (y * g.astype(jnp.float32) + b.astype(jnp.float32)).astype(x.dtype)


def dwconv(x, w):
    return lax.conv_general_dilated(
        x, w.astype(x.dtype)[:, None, :], window_strides=(1,), padding='SAME',
        dimension_numbers=('NWC', 'WIO', 'NWC'), feature_group_count=x.shape[-1])


def split_proj(p):
    widths = (CONV_A_WIDTH, CONV_A_WIDTH, CONV_A_WIDTH, 2 * CONV_B_WIDTH, Q_LORA, KV_LORA, QK_ROPE)
    points, acc = [], 0
    for w in widths[:-1]:
        acc += w
        points.append(acc)
    return jnp.split(p, points, axis=-1)


def conv_mixers(a_b, a_c, a_h, glu_in, conv_a_w, conv_b_w, conv_b_bias, ln_g, ln_b):
    y_a = a_b * dwconv(a_c * a_h, conv_a_w)
    g_a, g_b = jnp.split(glu_in, 2, axis=-1)
    u = dwconv(g_a * jax.nn.sigmoid(g_b), conv_b_w) + conv_b_bias
    y_b = jax.nn.silu(layernorm(u, ln_g, ln_b))
    return jnp.concatenate([y_a, y_b], axis=-1)


def mla_q(p_q, g, w_uq):
    q = (rmsnorm(p_q, g) @ w_uq).reshape(*p_q.shape[:-1], MLA_HEADS, QK_NOPE + QK_ROPE)
    return q[..., :QK_NOPE], q[..., QK_NOPE:]


def mla_kv(p_kv, g, w_ukv):
    kv = (rmsnorm(p_kv, g) @ w_ukv).reshape(*p_kv.shape[:-1], MLA_HEADS, QK_NOPE + V_DIM)
    return kv[..., :QK_NOPE], kv[..., QK_NOPE:]


def rotate_half(seg):
    h = seg.shape[-1] // 2
    return jnp.concatenate([-seg[..., h:], seg[..., :h]], axis=-1)


def apply_rope2d(x, cos, sin):
    half = QK_ROPE // 2
    rot = jnp.concatenate([rotate_half(x[..., :half]), rotate_half(x[..., half:])], axis=-1)
    return x * cos.astype(x.dtype) + rot * sin.astype(x.dtype)


def rope2d_tables(n):
    rows = n // GRID_W
    row = jnp.repeat(jnp.arange(rows, dtype=jnp.float32), GRID_W)
    col = jnp.tile(jnp.arange(GRID_W, dtype=jnp.float32), rows)
    axis_dim = QK_ROPE // 2
    inv = ROPE_BASE ** (-jnp.arange(0, axis_dim, 2, dtype=jnp.float32) / axis_dim)
    ar = row[:, None] * inv[None, :]
    ac = col[:, None] * inv[None, :]
    ang = jnp.concatenate([ar, ar, ac, ac], axis=-1)
    return jnp.cos(ang), jnp.sin(ang)


def join_qk(nope, rope_part):
    if rope_part.ndim == nope.ndim - 1:
        rope_part = jnp.broadcast_to(rope_part[..., None, :], nope.shape[:-1] + (QK_ROPE,))
    return jnp.concatenate([nope, rope_part], axis=-1)


def attend(q, k, v):
    s = jnp.einsum('bqhd,bkhd->bhqk', q, k).astype(jnp.float32) * ATTN_SCALE
    p = jax.nn.softmax(s, axis=-1)
    return jnp.einsum('bhqk,bkhd->bqhd', p.astype(v.dtype), v)


def latent_attention(q, k, v):
    b, n = q.shape[0], q.shape[1]
    nb = n // Q_BLOCK
    qb = q.reshape(b, nb, Q_BLOCK, MLA_HEADS, q.shape[-1]).transpose(1, 0, 2, 3, 4)
    out = lax.map(lambda qi: attend(qi, k, v), qb)
    return out.transpose(1, 0, 2, 3, 4).reshape(b, n, MLA_WIDTH)


def peer_ffn(h, w_q, sub_keys, u, v):
    b, l, d = h.shape
    hb = h.reshape(-1, PEER_CHUNK, d)

    def chunk(ht):
        q = (ht @ w_q).reshape(PEER_CHUNK, PEER_HEADS, 2, PK_HALF)
        s = jnp.einsum('thpd,hpkd->thpk', q, sub_keys).astype(jnp.float32)
        top_s, top_i = lax.top_k(s, PEER_TOPK)
        cand_s = (top_s[:, :, 0, :, None] + top_s[:, :, 1, None, :]).reshape(PEER_CHUNK, PEER_HEADS, -1)
        cand_i = (top_i[:, :, 0, :, None] * N_KEYS + top_i[:, :, 1, None, :]).reshape(PEER_CHUNK, PEER_HEADS, -1)
        best_s, best_p = lax.top_k(cand_s, PEER_TOPK)
        idx = jnp.take_along_axis(cand_i, best_p, axis=-1)
        gate = jax.nn.softmax(best_s, axis=-1)
        act = jnp.einsum('thkd,td->thk', u[idx], ht).astype(jnp.float32)
        wts = (jax.nn.gelu(act, approximate=False) * gate).astype(ht.dtype)
        return jnp.einsum('thk,thkd->td', wts, v[idx])

    return lax.map(chunk, hb).reshape(b, l, d)


def setup_inputs(seed: int = 0) -> dict:
    key = jax.random.key(seed)
    ks = jax.random.split(key, 24)
    f = jnp.float32
    nrm = lambda k, shape, s: jax.random.normal(k, shape, f) * s
    D = D_MODEL
    return {
        "x": nrm(ks[0], (BATCH, SEQ, D), 1.0),
        "c": nrm(ks[1], (BATCH, D), 1.0),
        "ctx": nrm(ks[2], (BATCH, CTX_LEN, D), 1.0),
        "c_ctx": nrm(ks[3], (D,), 1.0),
        "w_mod": nrm(ks[4], (DEPTH, D, 6 * D), 0.5 * D ** -0.5),
        "b_mod": nrm(ks[5], (DEPTH, 6 * D), 0.02),
        "norm1_g": 1.0 + nrm(ks[6], (DEPTH, D), 0.02),
        "norm2_g": 1.0 + nrm(ks[7], (DEPTH, D), 0.02),
        "w_in": nrm(ks[8], (DEPTH, D, PROJ_WIDTH), D ** -0.5),
        "w_out": nrm(ks[9], (DEPTH, MIX_WIDTH, D), MIX_WIDTH ** -0.5),
        "conv_a_w": nrm(ks[10], (DEPTH, CONV_A_K, CONV_A_WIDTH), CONV_A_K ** -0.5),
        "conv_b_w": nrm(ks[11], (DEPTH, CONV_B_K, CONV_B_WIDTH), CONV_B_K ** -0.5),
        "conv_b_bias": nrm(ks[12], (DEPTH, CONV_B_WIDTH), 0.02),
        "conv_b_ln_g": 1.0 + nrm(ks[13], (DEPTH, CONV_B_WIDTH), 0.02),
        "conv_b_ln_b": nrm(ks[14], (DEPTH, CONV_B_WIDTH), 0.02),
        "mla_q_norm_g": 1.0 + nrm(ks[15], (DEPTH, Q_LORA), 0.02),
        "w_uq": nrm(ks[16], (DEPTH, Q_LORA, MLA_HEADS * (QK_NOPE + QK_ROPE)), Q_LORA ** -0.5),
        "mla_kv_norm_g": 1.0 + nrm(ks[17], (DEPTH, KV_LORA), 0.02),
        "w_ukv": nrm(ks[18], (DEPTH, KV_LORA, MLA_HEADS * (QK_NOPE + V_DIM)), KV_LORA ** -0.5),
        "peer_w_q": nrm(ks[19], (DEPTH, D, PEER_HEADS * 2 * PK_HALF), D ** -0.5),
        "peer_sub_keys": nrm(ks[20], (DEPTH, PEER_HEADS, 2, N_KEYS, PK_HALF), PK_HALF ** -0.5),
        "peer_u": nrm(ks[21], (DEPTH, N_EXPERTS, D), D ** -0.5),
        "peer_v": nrm(ks[22], (DEPTH, N_EXPERTS, D), PEER_HEADS ** -0.5),
        "final_norm_g": 1.0 + nrm(ks[23], (D,), 0.02),
    }


def reference(x, c, ctx, c_ctx, w_mod, b_mod, norm1_g, norm2_g, w_in, w_out, conv_a_w, conv_b_w,
              conv_b_bias, conv_b_ln_g, conv_b_ln_b, mla_q_norm_g, w_uq, mla_kv_norm_g, w_ukv,
              peer_w_q, peer_sub_keys, peer_u, peer_v, final_norm_g):
    n = x.shape[1]
    cos, sin = rope2d_tables(n)
    cos_q, sin_q = cos[None, :, None, :], sin[None, :, None, :]
    cos_k, sin_k = cos[None, :, :], sin[None, :, :]
    silu_c = jax.nn.silu(c)
    silu_cc = jax.nn.silu(c_ctx)

    for l in range(DEPTH):
        last = l == DEPTH - 1
        sh1, sc1, g1, sh2, sc2, g2 = jnp.split((silu_c @ w_mod[l] + b_mod[l])[:, None, :], 6, axis=-1)
        csh1, csc1, cg1, csh2, csc2, cg2 = jnp.split(silu_cc @ w_mod[l] + b_mod[l], 6, axis=-1)

        hx = rmsnorm(x, norm1_g[l]) * (1 + sc1) + sh1
        hc = rmsnorm(ctx, norm1_g[l]) * (1 + csc1) + csh1
        xa_b, xa_c, xa_h, x_glu, x_q, x_kv, x_kr = split_proj(hx @ w_in[l])
        ca_b, ca_c, ca_h, c_glu, c_q, c_kv, c_kr = split_proj(hc @ w_in[l])

        kn_c, v_c = mla_kv(c_kv, mla_kv_norm_g[l], w_ukv[l])
        k_c = join_qk(kn_c, c_kr)

        qn_x, qr_x = mla_q(x_q, mla_q_norm_g[l], w_uq[l])
        kn_x, v_x = mla_kv(x_kv, mla_kv_norm_g[l], w_ukv[l])
        q_x = join_qk(qn_x, apply_rope2d(qr_x, cos_q, sin_q))
        k_x = join_qk(kn_x, apply_rope2d(x_kr, cos_k, sin_k))
        att_x = latent_attention(q_x, jnp.concatenate([k_x, k_c], axis=1), jnp.concatenate([v_x, v_c], axis=1))

        conv_x = conv_mixers(xa_b, xa_c, xa_h, x_glu, conv_a_w[l], conv_b_w[l], conv_b_bias[l],
                             conv_b_ln_g[l], conv_b_ln_b[l])
        x = x + g1 * (jnp.concatenate([conv_x, att_x], axis=-1) @ w_out[l])

        if not last:
            qn_c, qr_c = mla_q(c_q, mla_q_norm_g[l], w_uq[l])
            att_c = attend(join_qk(qn_c, qr_c), k_c, v_c).reshape(ctx.shape[0], ctx.shape[1], MLA_WIDTH)
            conv_c = conv_mixers(ca_b, ca_c, ca_h, c_glu, conv_a_w[l], conv_b_w[l], conv_b_bias[l],
                                 conv_b_ln_g[l], conv_b_ln_b[l])
            ctx = ctx + cg1 * (jnp.concatenate([conv_c, att_c], axis=-1) @ w_out[l])

        hx2 = rmsnorm(x, norm2_g[l]) * (1 + sc2) + sh2
        x = x + g2 * peer_ffn(hx2, peer_w_q[l], peer_sub_keys[l], peer_u[l], peer_v[l])
        if not last:
            hc2 = rmsnorm(ctx, norm2_g[l]) * (1 + csc2) + csh2
            ctx = ctx + cg2 * peer_ffn(hc2, peer_w_q[l], peer_sub_keys[l], peer_u[l], peer_v[l])

    return rmsnorm(x, final_norm_g)
```

```python
import functools
import math

import jax
import jax.numpy as jnp
import numpy as np
from jax import lax
from jax.experimental import pallas as pl
from jax.experimental.pallas import tpu as pltpu

F32 = jnp.float32
BF16 = jnp.bfloat16

D_MODEL = 1024
GRID_W = 64
CONV_W = 256
CONV_A_K = 3
CONV_B_K = 31
HEADS = 8
QK_NOPE = 64
QK_ROPE = 32
V_DIM = 64
Q_LORA = 256
KV_LORA = 128
HEAD_PAD = 128
ATTN_SCALE = (QK_NOPE + QK_ROPE) ** -0.5
ROPE_BASE = 10000.0
PEER_HEADS = 8
N_KEYS = 128
PK_HALF = 128
PEER_TOPK = 16
EPS = 1e-6

LANES = 128
CONV_HALO = 16
N_CAND = 50
N_CAND_PAD = 56
NOT_RANKED = 99.0

PROJ_COLS = 5 * CONV_W + Q_LORA + KV_LORA + 2 * LANES
VMEM_LIMIT = 56 * 1024 * 1024


def _cparams(sem):
    return pltpu.CompilerParams(dimension_semantics=sem, vmem_limit_bytes=VMEM_LIMIT)


def _rms(x, g):
    return x * lax.rsqrt(jnp.mean(x * x, axis=-1, keepdims=True) + EPS) * g


def _sigmoid(x):
    return 1.0 / (1.0 + jnp.exp(-x))


def _mod_kernel(c_ref, w_ref, b_ref, o_ref):
    c = c_ref[...]
    o_ref[0] = jnp.dot(c * _sigmoid(c), w_ref[0], preferred_element_type=F32) + b_ref[0]


def _modulation(cvec, w_mod, b_mod):
    depth, d, n = w_mod.shape
    r = cvec.shape[0]
    nt = 1536
    return pl.pallas_call(
        _mod_kernel,
        grid=(depth, n // nt),
        in_specs=[
            pl.BlockSpec((r, d), lambda l, j: (0, 0)),
            pl.BlockSpec((1, d, nt), lambda l, j: (l, 0, j)),
            pl.BlockSpec((1, 1, nt), lambda l, j: (l, 0, j)),
        ],
        out_specs=pl.BlockSpec((1, r, nt), lambda l, j: (l, 0, j)),
        out_shape=jax.ShapeDtypeStruct((depth, r, n), F32),
        compiler_params=_cparams(("arbitrary", "arbitrary")),
        name="modulation",
    )(cvec, w_mod, b_mod.reshape(depth, 1, n))


def _inproj_kernel(x_ref, sh_ref, sc_ref, g_ref, win_ref, gq_ref, wq_ref, wqr_ref, gkv_ref, wk_ref,
                   wv_ref, cq_ref, sq_ref, ck_ref, conv_ref, q_ref, k_ref, v_ref):
    x = x_ref[0]
    h = _rms(x, g_ref[...]) * (1.0 + sc_ref[0]) + sh_ref[0]
    p = jnp.dot(h.astype(BF16), win_ref[...], preferred_element_type=F32)
    w = CONV_W
    conv_ref[0, :, 0:w] = p[:, 0:w]
    conv_ref[0, :, w:2 * w] = p[:, w:2 * w] * p[:, 2 * w:3 * w]
    conv_ref[0, :, 2 * w:3 * w] = p[:, 3 * w:4 * w] * _sigmoid(p[:, 4 * w:5 * w])
    o = 5 * w
    hq = _rms(p[:, o:o + Q_LORA], gq_ref[...]).astype(BF16)
    o += Q_LORA
    hkv = _rms(p[:, o:o + KV_LORA], gkv_ref[...]).astype(BF16)
    o += KV_LORA
    kr_a = p[:, o:o + LANES]
    kr_b = p[:, o + LANES:o + 2 * LANES]
    cq = cq_ref[...]
    sq = sq_ref[...]
    qa = jnp.dot(hq, wq_ref[...], preferred_element_type=F32)
    qb = jnp.dot(hq, wqr_ref[...], preferred_element_type=F32)
    kn = jnp.dot(hkv, wk_ref[...], preferred_element_type=F32)
    kr = kr_a * ck_ref[...] + kr_b * sq
    for hd in range(HEADS):
        ls = slice(hd * HEAD_PAD, (hd + 1) * HEAD_PAD)
        q_ref[0, :, ls] = (qa[:, ls] * cq + qb[:, ls] * sq).astype(BF16)
        k_ref[0, :, ls] = (kn[:, ls] + kr).astype(BF16)
    v_ref[0] = jnp.dot(hkv, wv_ref[...], preferred_element_type=F32).astype(BF16)


def _in_projection(x, sh, sc, g, lw, tabs, ts):
    b, s, d = x.shape
    cq, sq, ck = tabs
    hp = HEADS * HEAD_PAD
    const = lambda shape: pl.BlockSpec(shape, lambda i, j: (0,) * len(shape))
    mod_spec = pl.BlockSpec((1, 1, d), (lambda i, j: (i, 0, 0)) if sh.shape[0] > 1 else (lambda i, j: (0, 0, 0)))
    tab_spec = pl.BlockSpec((ts, LANES), lambda i, j: (j, 0))
    row_spec = lambda wd: pl.BlockSpec((1, ts, wd), lambda i, j: (i, j, 0))
    return pl.pallas_call(
        _inproj_kernel,
        grid=(b, s // ts),
        in_specs=[row_spec(d), mod_spec, mod_spec, const((1, d)), const((d, PROJ_COLS)),
                  const((1, Q_LORA)), const((Q_LORA, hp)), const((Q_LORA, hp)),
                  const((1, KV_LORA)), const((KV_LORA, hp)), const((KV_LORA, hp)),
                  tab_spec, tab_spec, tab_spec],
        out_specs=[row_spec(3 * CONV_W), row_spec(hp), row_spec(hp), row_spec(hp)],
        out_shape=[jax.ShapeDtypeStruct((b, s, 3 * CONV_W), F32),
                   jax.ShapeDtypeStruct((b, s, hp), BF16),
                   jax.ShapeDtypeStruct((b, s, hp), BF16),
                   jax.ShapeDtypeStruct((b, s, hp), BF16)],
        compiler_params=_cparams(("arbitrary", "arbitrary")),
        name="in_projection",
    )(x, sh, sc, g, lw["w_in"], lw["gq"], lw["wq"], lw["wq_rot"], lw["gkv"], lw["wk"], lw["wv"], cq, sq, ck)


def _conv_kernel(rc, in_ref, wa_ref, wb_ref, bias_ref, lng_ref, lnb_ref, out_ref, pad_ref):
    s = in_ref.shape[1]
    w = CONV_W
    zeros = jnp.zeros((CONV_HALO, 2 * w), F32)
    pad_ref[0:CONV_HALO, :] = zeros
    pad_ref[CONV_HALO + s:2 * CONV_HALO + s, :] = zeros
    pad_ref[CONV_HALO:CONV_HALO + s, :] = in_ref[0, :, w:3 * w]
    wa = wa_ref[...]
    wb = wb_ref[...]
    for r0 in range(0, s, rc):
        acc_a = jnp.zeros((rc, w), F32)
        for k in range(CONV_A_K):
            st = CONV_HALO + r0 + k - CONV_A_K // 2
            acc_a = acc_a + wa[k:k + 1, :] * pad_ref[st:st + rc, 0:w]
        acc_b = jnp.zeros((rc, w), F32)
        for k in range(CONV_B_K):
            st = CONV_HALO + r0 + k - CONV_B_K // 2
            acc_b = acc_b + wb[k:k + 1, :] * pad_ref[st:st + rc, w:2 * w]
        u = acc_b + bias_ref[...]
        mu = jnp.mean(u, axis=-1, keepdims=True)
        var = jnp.mean(jnp.square(u - mu), axis=-1, keepdims=True)
        y = (u - mu) * lax.rsqrt(var + EPS) * lng_ref[...] + lnb_ref[...]
        out_ref[0, r0:r0 + rc, 0:w] = (in_ref[0, r0:r0 + rc, 0:w] * acc_a).astype(BF16)
        out_ref[0, r0:r0 + rc, w:2 * w] = (y * _sigmoid(y)).astype(BF16)


def _conv_mixers(cin, lw):
    b, s, _ = cin.shape
    w = CONV_W
    const = lambda shape: pl.BlockSpec(shape, lambda i: (0,) * len(shape))
    return pl.pallas_call(
        functools.partial(_conv_kernel, min(s, 128)),
        grid=(b,),
        in_specs=[pl.BlockSpec((1, s, 3 * w), lambda i: (i, 0, 0)),
                  const((CONV_A_K, w)), const((CONV_B_K, w)), const((1, w)), const((1, w)), const((1, w))],
        out_specs=pl.BlockSpec((1, s, 2 * w), lambda i: (i, 0, 0)),
        out_shape=jax.ShapeDtypeStruct((b, s, 2 * w), BF16),
        scratch_shapes=[pltpu.VMEM((s + 2 * CONV_HALO, 2 * w), F32)],
        compiler_params=_cparams(("arbitrary",)),
        name="conv_mixers",
    )(cin, lw["conv_a_w"], lw["conv_b_w"], lw["conv_b_bias"], lw["ln_g"], lw["ln_b"])


def _attn_kernel(nkv, *refs):
    q_ref = refs[0]
    kv_refs = refs[1:1 + 2 * nkv]
    conv_ref, x_ref, g_ref, wo_ref, o_ref, att_ref = refs[1 + 2 * nkv:]
    nt_dims = (((1,), (1,)), ((), ()))
    for hd in range(HEADS):
        ls = slice(hd * HEAD_PAD, (hd + 1) * HEAD_PAD)
        qh = q_ref[0, :, ls]
        ss = [lax.dot_general(qh, kv_refs[2 * i][0, :, ls], nt_dims, preferred_element_type=F32)
              for i in range(nkv)]
        m = ss[0].max(axis=-1, keepdims=True)
        for sx in ss[1:]:
            m = jnp.maximum(m, sx.max(axis=-1, keepdims=True))
        den = None
        num = None
        for i, sx in enumerate(ss):
            p = jnp.exp((sx - m) * ATTN_SCALE)
            d_i = p.sum(axis=-1, keepdims=True)
            n_i = jnp.dot(p.astype(BF16), kv_refs[2 * i + 1][0, :, ls], preferred_element_type=F32)
            den = d_i if den is None else den + d_i
            num = n_i if num is None else num + n_i
        o = num / den
        att_ref[:, hd * V_DIM:(hd + 1) * V_DIM] = o[:, 0:V_DIM].astype(BF16)
    half = HEADS * V_DIM
    mix = jnp.dot(conv_ref[0], wo_ref[0:half, :], preferred_element_type=F32)
    mix = mix + jnp.dot(att_ref[...], wo_ref[half:2 * half, :], preferred_element_type=F32)
    o_ref[0] = x_ref[0] + g_ref[0] * mix


def _attention_block(q, kvs, conv, x, gate, w_out, tq):
    b, s, d = x.shape
    hp = HEADS * HEAD_PAD
    nkv = len(kvs) // 2
    row_spec = lambda wd: pl.BlockSpec((1, tq, wd), lambda i, j: (i, j, 0))
    kv_specs = [pl.BlockSpec((1, a.shape[1], hp), lambda i, j: (i, 0, 0)) for a in kvs]
    gate_spec = pl.BlockSpec((1, 1, d), (lambda i, j: (i, 0, 0)) if gate.shape[0] > 1 else (lambda i, j: (0, 0, 0)))
    return pl.pallas_call(
        functools.partial(_attn_kernel, nkv),
        grid=(b, s // tq),
        in_specs=[row_spec(hp)] + kv_specs + [row_spec(2 * CONV_W), row_spec(d), gate_spec,
                                              pl.BlockSpec((d, d), lambda i, j: (0, 0))],
        out_specs=row_spec(d),
        out_shape=jax.ShapeDtypeStruct((b, s, d), F32),
        scratch_shapes=[pltpu.VMEM((tq, HEADS * V_DIM), BF16)],
        compiler_params=_cparams(("arbitrary", "arbitrary")),
        name="attention_out_projection",
    )(q, *kvs, conv, x, gate, w_out)


def _top_ranks(s):
    iota = lax.broadcasted_iota(jnp.int32, s.shape, 0).astype(F32)
    rank = jnp.full(s.shape, NOT_RANKED, F32)
    vals = []
    for j in range(PEER_TOPK):
        m = jnp.max(s, axis=0, keepdims=True)
        idx = jnp.min(jnp.where(s == m, iota, float(N_KEYS)), axis=0, keepdims=True)
        sel = iota == idx
        rank = jnp.where(sel, float(j + 1), rank)
        s = jnp.where(sel, -jnp.inf, s)
        vals.append(m)
    return rank, jnp.concatenate(vals, axis=0)


def _pair_selection(a, b, ea, eb, cand_ref, cexp_ref):
    off = 0
    offs = []
    for r1 in range(PEER_TOPK):
        n = PEER_TOPK // (r1 + 1)
        cand_ref[off:off + n, :] = a[r1:r1 + 1, :] + b[0:n, :]
        cexp_ref[off:off + n, :] = ea[r1:r1 + 1, :] * eb[0:n, :]
        offs.append((off, n))
        off += n
    cand_ref[N_CAND:N_CAND_PAD, :] = jnp.full((N_CAND_PAD - N_CAND, LANES), -jnp.inf, F32)
    cexp_ref[N_CAND:N_CAND_PAD, :] = jnp.zeros((N_CAND_PAD - N_CAND, LANES), F32)
    cand = cand_ref[...]
    iota = lax.broadcasted_iota(jnp.int32, cand.shape, 0).astype(F32)
    chosen = jnp.zeros(cand.shape, F32)
    for _ in range(PEER_TOPK):
        m = jnp.max(cand, axis=0, keepdims=True)
        pos = jnp.min(jnp.where(cand == m, iota, float(N_CAND_PAD)), axis=0, keepdims=True)
        sel = iota == pos
        chosen = jnp.where(sel, 1.0, chosen)
        cand = jnp.where(sel, -jnp.inf, cand)
    z = jnp.sum(chosen * cexp_ref[...], axis=0, keepdims=True)
    cand_ref[...] = chosen
    counts = [jnp.sum(cand_ref[o:o + n, :], axis=0, keepdims=True) for (o, n) in offs]
    return counts, z


def _peer_kernel(t, ch, final, *refs):
    if final:
        (x_ref, sh_ref, sc_ref, g_ref, ng_ref, wqt_ref, sk_ref, u_ref, vt_ref, fg_ref, o_ref,
         ht_s, q_s, r2_s, e2_s, m_s, c_s, cand_s, cexp_s, act_s, w_s, acc_s) = refs
    else:
        (x_ref, sh_ref, sc_ref, g_ref, ng_ref, wqt_ref, sk_ref, u_ref, vt_ref, o_ref,
         ht_s, q_s, r2_s, e2_s, m_s, c_s, cand_s, cexp_s, act_s, w_s, acc_s) = refs
        fg_ref = None
    c = pl.program_id(1)
    n_lg = t // LANES
    i1_per_chunk = ch // N_KEYS

    @pl.when(c == 0)
    def _prologue():
        h = _rms(x_ref[...], ng_ref[...]) * (1.0 + sc_ref[0]) + sh_ref[0]
        ht_s[...] = h.T.astype(BF16)
        q_s[...] = jnp.dot(wqt_ref[...], ht_s[...], preferred_element_type=F32).astype(BF16)

        def head_body(hd, carry):
            base = pl.multiple_of(hd * (2 * PK_HALF), 2 * PK_HALF)
            s1 = jnp.dot(sk_ref[hd, 0], q_s[pl.ds(base, PK_HALF), :], preferred_element_type=F32)
            s2 = jnp.dot(sk_ref[hd, 1], q_s[pl.ds(base + PK_HALF, PK_HALF), :], preferred_element_type=F32)
            for lg in range(n_lg):
                ls = slice(lg * LANES, (lg + 1) * LANES)
                s1b = s1[:, ls]
                s2b = s2[:, ls]
                r1, a = _top_ranks(s1b)
                r2, b = _top_ranks(s2b)
                e1 = jnp.exp(s1b - a[0:1, :])
                e2 = jnp.exp(s2b - b[0:1, :])
                ea = jnp.exp(a - a[0:1, :])
                eb = jnp.exp(b - b[0:1, :])
                counts, z = _pair_selection(a, b, ea, eb, cand_s, cexp_s)
                mm = jnp.zeros(r1.shape, F32)
                for j in range(PEER_TOPK):
                    mm = jnp.where(r1 == float(j + 1), counts[j], mm)
                r2_s[hd, :, ls] = r2
                e2_s[hd, :, ls] = e2
                m_s[hd, :, ls] = mm
                c_s[hd, :, ls] = e1 / z
            return carry

        lax.fori_loop(0, PEER_HEADS, head_body, 0)

    act_s[...] = jnp.dot(u_ref[...], ht_s[...], preferred_element_type=F32)

    i1_base = pl.multiple_of(c * i1_per_chunk, 8)
    for i1l in range(i1_per_chunk):
        row0 = i1l * N_KEYS
        grp = pl.ds(i1_base + (i1l // 8) * 8, 8)
        sub = i1l % 8
        for lg in range(n_lg):
            ls = slice(lg * LANES, (lg + 1) * LANES)
            m_rows = [jnp.broadcast_to(m_s[hd, grp, ls][sub:sub + 1, :], (16, LANES)) for hd in range(PEER_HEADS)]
            c_rows = [jnp.broadcast_to(c_s[hd, grp, ls][sub:sub + 1, :], (16, LANES)) for hd in range(PEER_HEADS)]

            def rg_body(rg, carry, row0=row0, ls=ls, m_rows=m_rows, c_rows=c_rows):
                r0 = pl.multiple_of(rg * 16, 16)
                gate = jnp.zeros((16, LANES), F32)
                for hd in range(PEER_HEADS):
                    r2 = r2_s[hd, pl.ds(r0, 16), ls]
                    e2 = e2_s[hd, pl.ds(r0, 16), ls]
                    gate = gate + jnp.where(r2 <= m_rows[hd], e2, 0.0) * c_rows[hd]
                a = act_s[pl.ds(row0 + r0, 16), ls]
                gelu = a * (lax.erf(a * (1.0 / math.sqrt(2.0))) + 1.0) * 0.5
                w_s[pl.ds(row0 + r0, 16), ls] = (gelu * gate).astype(BF16)
                return carry

            lax.fori_loop(0, N_KEYS // 16, rg_body, 0)

    contrib = jnp.dot(vt_ref[...], w_s[...], preferred_element_type=F32)

    @pl.when(c == 0)
    def _first():
        acc_s[...] = contrib

    @pl.when(c > 0)
    def _rest():
        acc_s[...] = acc_s[...] + contrib

    @pl.when(c == pl.num_programs(1) - 1)
    def _epilogue():
        y = x_ref[...] + g_ref[0] * acc_s[...].T
        if final:
            y = _rms(y, fg_ref[...])
        o_ref[...] = y


def _peer_block(x2, sh, sc, gate, lw, tiles_per_mod, final_g, t, ch):
    n, d = x2.shape
    n_exp = N_KEYS * N_KEYS
    final = final_g is not None
    const = lambda shape: pl.BlockSpec(shape, lambda i, c: (0,) * len(shape))
    if sh.shape[0] > 1:
        mod_spec = pl.BlockSpec((1, 1, d), lambda i, c: (i // tiles_per_mod, 0, 0))
    else:
        mod_spec = pl.BlockSpec((1, 1, d), lambda i, c: (0, 0, 0))
    in_specs = [pl.BlockSpec((t, d), lambda i, c: (i, 0)), mod_spec, mod_spec, mod_spec, const((1, d)),
                const((PEER_HEADS * 2 * PK_HALF, d)), const((PEER_HEADS, 2, N_KEYS, PK_HALF)),
                pl.BlockSpec((ch, d), lambda i, c: (c, 0)), pl.BlockSpec((d, ch), lambda i, c: (0, c))]
    args = [x2, sh, sc, gate, lw["norm2_g"], lw["peer_wqt"], lw["peer_sk"], lw["peer_u"], lw["peer_vt"]]
    if final:
        in_specs.append(const((1, d)))
        args.append(final_g)
    tab = pltpu.VMEM((PEER_HEADS, N_KEYS, t), F32)
    return pl.pallas_call(
        functools.partial(_peer_kernel, t, ch, final),
        grid=(n // t, n_exp // ch),
        in_specs=in_specs,
        out_specs=pl.BlockSpec((t, d), lambda i, c: (i, 0)),
        out_shape=jax.ShapeDtypeStruct((n, d), F32),
        scratch_shapes=[pltpu.VMEM((d, t), BF16),
                        pltpu.VMEM((PEER_HEADS * 2 * PK_HALF, t), BF16),
                        tab, tab, tab, tab,
                        pltpu.VMEM((N_CAND_PAD, LANES), F32), pltpu.VMEM((N_CAND_PAD, LANES), F32),
                        pltpu.VMEM((ch, t), F32),
                        pltpu.VMEM((ch, t), BF16),
                        pltpu.VMEM((d, t), F32)],
        compiler_params=_cparams(("arbitrary", "arbitrary")),
        name="peer",
    )(*args)


def _rot_cols(w):
    q = QK_ROPE // 4
    return jnp.concatenate([-w[:, q:2 * q], w[:, 0:q], -w[:, 3 * q:4 * q], w[:, 2 * q:3 * q]], axis=-1)


def _layer_weights(l, w_in, w_out, conv_a_w, conv_b_w, conv_b_bias, conv_b_ln_g, conv_b_ln_b, mla_q_norm_g,
                   w_uq, mla_kv_norm_g, w_ukv, norm2_g, peer_w_q, peer_sub_keys, peer_u, peer_v):
    d = D_MODEL
    wi = w_in[l]
    o = 5 * CONV_W + Q_LORA + KV_LORA
    w_kr = wi[:, o:o + QK_ROPE]
    zpad = lambda n: jnp.zeros((d, n), F32)
    kr_a = jnp.concatenate([zpad(QK_NOPE), w_kr, zpad(LANES - QK_NOPE - QK_ROPE)], axis=-1)
    kr_b = jnp.concatenate([zpad(QK_NOPE), _rot_cols(w_kr), zpad(LANES - QK_NOPE - QK_ROPE)], axis=-1)
    w_in_ext = jnp.concatenate([wi[:, :o], kr_a, kr_b], axis=-1).astype(BF16)

    wq3 = w_uq[l].reshape(Q_LORA, HEADS, QK_NOPE + QK_ROPE)
    zq = jnp.zeros((Q_LORA, HEADS, HEAD_PAD - QK_NOPE - QK_ROPE), F32)
    wq = jnp.concatenate([wq3, zq], axis=-1).reshape(Q_LORA, HEADS * HEAD_PAD).astype(BF16)
    rot = _rot_cols(wq3[:, :, QK_NOPE:].reshape(Q_LORA * HEADS, QK_ROPE)).reshape(Q_LORA, HEADS, QK_ROPE)
    wq_rot = jnp.concatenate([jnp.zeros((Q_LORA, HEADS, QK_NOPE), F32), rot, zq], axis=-1)
    wq_rot = wq_rot.reshape(Q_LORA, HEADS * HEAD_PAD).astype(BF16)

    wkv3 = w_ukv[l].reshape(KV_LORA, HEADS, QK_NOPE + V_DIM)
    zk = jnp.zeros((KV_LORA, HEADS, HEAD_PAD - QK_NOPE), F32)
    wk = jnp.concatenate([wkv3[:, :, :QK_NOPE], zk], axis=-1).reshape(KV_LORA, HEADS * HEAD_PAD).astype(BF16)
    zv = jnp.zeros((KV_LORA, HEADS, HEAD_PAD - V_DIM), F32)
    wv = jnp.concatenate([wkv3[:, :, QK_NOPE:], zv], axis=-1).reshape(KV_LORA, HEADS * HEAD_PAD).astype(BF16)

    return dict(
        w_in=w_in_ext, wq=wq, wq_rot=wq_rot, wk=wk, wv=wv,
        gq=mla_q_norm_g[l].reshape(1, Q_LORA), gkv=mla_kv_norm_g[l].reshape(1, KV_LORA),
        w_out=w_out[l].astype(BF16),
        conv_a_w=conv_a_w[l], conv_b_w=conv_b_w[l], conv_b_bias=conv_b_bias[l].reshape(1, CONV_W),
        ln_g=conv_b_ln_g[l].reshape(1, CONV_W), ln_b=conv_b_ln_b[l].reshape(1, CONV_W),
        norm2_g=norm2_g[l].reshape(1, d),
        peer_wqt=peer_w_q[l].T.astype(BF16),
        peer_sk=peer_sub_keys[l].astype(BF16),
        peer_u=peer_u[l].astype(BF16),
        peer_vt=peer_v[l].T.astype(BF16),
    )


def _rope_tables(n):
    rows = n // GRID_W
    row = jnp.repeat(jnp.arange(rows, dtype=F32), GRID_W)
    col = jnp.tile(jnp.arange(GRID_W, dtype=F32), rows)
    axis_dim = QK_ROPE // 2
    inv = ROPE_BASE ** (-jnp.arange(0, axis_dim, 2, dtype=F32) / axis_dim)
    ar = row[:, None] * inv[None, :]
    ac = col[:, None] * inv[None, :]
    ang = jnp.concatenate([ar, ar, ac, ac], axis=-1)
    cos, sin = jnp.cos(ang), jnp.sin(ang)
    tail = jnp.zeros((n, LANES - QK_NOPE - QK_ROPE), F32)
    cq = jnp.concatenate([jnp.ones((n, QK_NOPE), F32), cos, tail], axis=-1)
    sq = jnp.concatenate([jnp.zeros((n, QK_NOPE), F32), sin, tail], axis=-1)
    ck = jnp.concatenate([jnp.zeros((n, QK_NOPE), F32), cos, tail], axis=-1)
    return cq, sq, ck


def _identity_tables(n):
    tail = jnp.zeros((n, LANES - QK_NOPE - QK_ROPE), F32)
    ones = jnp.ones((n, QK_ROPE), F32)
    cq = jnp.concatenate([jnp.ones((n, QK_NOPE), F32), ones, tail], axis=-1)
    ck = jnp.concatenate([jnp.zeros((n, QK_NOPE), F32), ones, tail], axis=-1)
    return cq, jnp.zeros((n, LANES), F32), ck


def kernel(x, c, ctx, c_ctx, w_mod, b_mod, norm1_g, norm2_g, w_in, w_out, conv_a_w, conv_b_w, conv_b_bias,
           conv_b_ln_g, conv_b_ln_b, mla_q_norm_g, w_uq, mla_kv_norm_g, w_ukv, peer_w_q, peer_sub_keys,
           peer_u, peer_v, final_norm_g):
    b, s, d = x.shape
    n_ctx = ctx.shape[1]
    depth = w_mod.shape[0]
    ts = min(s, 512)
    ts_c = min(n_ctx, 512)
    tq = min(s, 256)
    tq_c = min(n_ctx, 256)
    t_peer = min(s, 512)
    t_peer_c = min(b * n_ctx, 512)
    ch = 1024

    rows = ((b + 1 + 7) // 8) * 8
    cvec = jnp.concatenate([c, c_ctx[None, :], jnp.zeros((rows - b - 1, d), F32)], axis=0)
    mods = _modulation(cvec, w_mod, b_mod)
    tabs_x = _rope_tables(s)
    tabs_c = _identity_tables(n_ctx)

    for l in range(depth):
        last = l == depth - 1
        lw = _layer_weights(l, w_in, w_out, conv_a_w, conv_b_w, conv_b_bias, conv_b_ln_g, conv_b_ln_b,
                            mla_q_norm_g, w_uq, mla_kv_norm_g, w_ukv, norm2_g, peer_w_q, peer_sub_keys,
                            peer_u, peer_v)
        mx = [mods[l, :b, i * d:(i + 1) * d].reshape(b, 1, d) for i in range(6)]
        mc = [mods[l, b:b + 1, i * d:(i + 1) * d].reshape(1, 1, d) for i in range(6)]
        g1 = norm1_g[l].reshape(1, d)

        conv_in_x, q_x, k_x, v_x = _in_projection(x, mx[0], mx[1], g1, lw, tabs_x, ts)
        conv_in_c, q_c, k_c, v_c = _in_projection(ctx, mc[0], mc[1], g1, lw, tabs_c, ts_c)
        conv_x = _conv_mixers(conv_in_x, lw)
        x = _attention_block(q_x, [k_x, v_x, k_c, v_c], conv_x, x, mx[2], lw["w_out"], tq)
        if not last:
            conv_c = _conv_mixers(conv_in_c, lw)
            ctx = _attention_block(q_c, [k_c, v_c], conv_c, ctx, mc[2], lw["w_out"], tq_c)

        fg = final_norm_g.reshape(1, d) if last else None
        x = _peer_block(x.reshape(b * s, d), mx[3], mx[4], mx[5], lw, s // t_peer, fg, t_peer, ch).reshape(b, s, d)
        if not last:
            ctx = _peer_block(ctx.reshape(b * n_ctx, d), mc[3], mc[4], mc[5], lw, 1, None, t_peer_c, ch)
            ctx = ctx.reshape(b, n_ctx, d)
    return x
```

```python
import functools
import math

import jax
import jax.numpy as jnp
import numpy as np
from jax import lax
from jax.experimental import pallas as pl
from jax.experimental.pallas import tpu as pltpu

F32 = jnp.float32
BF16 = jnp.bfloat16

D_MODEL = 1024
GRID_W = 64
CONV_W = 256
CONV_A_K = 3
CONV_B_K = 31
HEADS = 8
QK_NOPE = 64
QK_ROPE = 32
V_DIM = 64
Q_LORA = 256
KV_LORA = 128
HEAD_PAD = 128
ATTN_SCALE = (QK_NOPE + QK_ROPE) ** -0.5
ROPE_BASE = 10000.0
PEER_HEADS = 8
N_KEYS = 128
PK_HALF = 128
PEER_TOPK = 16
EPS = 1e-6

LANES = 128
CONV_HALO = 16
N_CAND = 50
N_CAND_PAD = 56
NOT_RANKED = 99.0

PROJ_COLS = 5 * CONV_W + Q_LORA + KV_LORA + 2 * LANES
VMEM_LIMIT = 56 * 1024 * 1024


def _cparams(sem):
    return pltpu.CompilerParams(dimension_semantics=sem, vmem_limit_bytes=VMEM_LIMIT)


def _rms(x, g):
    return x * lax.rsqrt(jnp.mean(x * x, axis=-1, keepdims=True) + EPS) * g


def _sigmoid(x):
    return 1.0 / (1.0 + jnp.exp(-x))


def _mod_kernel(c_ref, w_ref, b_ref, o_ref):
    c = c_ref[...]
    o_ref[0] = jnp.dot(c * _sigmoid(c), w_ref[0], preferred_element_type=F32) + b_ref[0]


def _modulation(cvec, w_mod, b_mod):
    depth, d, n = w_mod.shape
    r = cvec.shape[0]
    nt = 1536
    return pl.pallas_call(
        _mod_kernel,
        grid=(depth, n // nt),
        in_specs=[
            pl.BlockSpec((r, d), lambda l, j: (0, 0)),
            pl.BlockSpec((1, d, nt), lambda l, j: (l, 0, j)),
            pl.BlockSpec((1, 1, nt), lambda l, j: (l, 0, j)),
        ],
        out_specs=pl.BlockSpec((1, r, nt), lambda l, j: (l, 0, j)),
        out_shape=jax.ShapeDtypeStruct((depth, r, n), F32),
        compiler_params=_cparams(("arbitrary", "arbitrary")),
        name="modulation",
    )(cvec, w_mod, b_mod.reshape(depth, 1, n))


def _inproj_kernel(x_ref, sh_ref, sc_ref, g_ref, win_ref, gq_ref, wq_ref, wqr_ref, gkv_ref, wk_ref,
                   wv_ref, cq_ref, sq_ref, ck_ref, conv_ref, q_ref, k_ref, v_ref):
    x = x_ref[0]
    h = _rms(x, g_ref[...]) * (1.0 + sc_ref[0]) + sh_ref[0]
    p = jnp.dot(h.astype(BF16), win_ref[...], preferred_element_type=F32)
    w = CONV_W
    conv_ref[0, :, 0:w] = p[:, 0:w]
    conv_ref[0, :, w:2 * w] = p[:, w:2 * w] * p[:, 2 * w:3 * w]
    conv_ref[0, :, 2 * w:3 * w] = p[:, 3 * w:4 * w] * _sigmoid(p[:, 4 * w:5 * w])
    o = 5 * w
    hq = _rms(p[:, o:o + Q_LORA], gq_ref[...]).astype(BF16)
    o += Q_LORA
    hkv = _rms(p[:, o:o + KV_LORA], gkv_ref[...]).astype(BF16)
    o += KV_LORA
    kr_a = p[:, o:o + LANES]
    kr_b = p[:, o + LANES:o + 2 * LANES]
    cq = cq_ref[...]
    sq = sq_ref[...]
    qa = jnp.dot(hq, wq_ref[...], preferred_element_type=F32)
    qb = jnp.dot(hq, wqr_ref[...], preferred_element_type=F32)
    kn = jnp.dot(hkv, wk_ref[...], preferred_element_type=F32)
    kr = kr_a * ck_ref[...] + kr_b * sq
    for hd in range(HEADS):
        ls = slice(hd * HEAD_PAD, (hd + 1) * HEAD_PAD)
        q_ref[0, :, ls] = (qa[:, ls] * cq + qb[:, ls] * sq).astype(BF16)
        k_ref[0, :, ls] = (kn[:, ls] + kr).astype(BF16)
    v_ref[0] = jnp.dot(hkv, wv_ref[...], preferred_element_type=F32).astype(BF16)


def _in_projection(x, sh, sc, g, lw, tabs, ts):
    b, s, d = x.shape
    cq, sq, ck = tabs
    hp = HEADS * HEAD_PAD
    const = lambda shape: pl.BlockSpec(shape, lambda i, j: (0,) * len(shape))
    mod_spec = pl.BlockSpec((1, 1, d), (lambda i, j: (i, 0, 0)) if sh.shape[0] > 1 else (lambda i, j: (0, 0, 0)))
    tab_spec = pl.BlockSpec((ts, LANES), lambda i, j: (j, 0))
    row_spec = lambda wd: pl.BlockSpec((1, ts, wd), lambda i, j: (i, j, 0))
    return pl.pallas_call(
        _inproj_kernel,
        grid=(b, s // ts),
        in_specs=[row_spec(d), mod_spec, mod_spec, const((1, d)), const((d, PROJ_COLS)),
                  const((1, Q_LORA)), const((Q_LORA, hp)), const((Q_LORA, hp)),
                  const((1, KV_LORA)), const((KV_LORA, hp)), const((KV_LORA, hp)),
                  tab_spec, tab_spec, tab_spec],
        out_specs=[row_spec(3 * CONV_W), row_spec(hp), row_spec(hp), row_spec(hp)],
        out_shape=[jax.ShapeDtypeStruct((b, s, 3 * CONV_W), F32),
                   jax.ShapeDtypeStruct((b, s, hp), BF16),
                   jax.ShapeDtypeStruct((b, s, hp), BF16),
                   jax.ShapeDtypeStruct((b, s, hp), BF16)],
        compiler_params=_cparams(("arbitrary", "arbitrary")),
        name="in_projection",
    )(x, sh, sc, g, lw["w_in"], lw["gq"], lw["wq"], lw["wq_rot"], lw["gkv"], lw["wk"], lw["wv"], cq, sq, ck)


def _conv_kernel(rc, in_ref, wa_ref, wb_ref, bias_ref, lng_ref, lnb_ref, out_ref, pad_ref):
    s = in_ref.shape[1]
    w = CONV_W
    zeros = jnp.zeros((CONV_HALO, 2 * w), F32)
    pad_ref[0:CONV_HALO, :] = zeros
    pad_ref[CONV_HALO + s:2 * CONV_HALO + s, :] = zeros
    pad_ref[CONV_HALO:CONV_HALO + s, :] = in_ref[0, :, w:3 * w]
    wa = wa_ref[...]
    wb = wb_ref[...]
    for r0 in range(0, s, rc):
        acc_a = jnp.zeros((rc, w), F32)
        for k in range(CONV_A_K):
            st = CONV_HALO + r0 + k - CONV_A_K // 2
            acc_a = acc_a + wa[k:k + 1, :] * pad_ref[st:st + rc, 0:w]
        acc_b = jnp.zeros((rc, w), F32)
        for k in range(CONV_B_K):
            st = CONV_HALO + r0 + k - CONV_B_K // 2
            acc_b = acc_b + wb[k:k + 1, :] * pad_ref[st:st + rc, w:2 * w]
        u = acc_b + bias_ref[...]
        mu = jnp.mean(u, axis=-1, keepdims=True)
        var = jnp.mean(jnp.square(u - mu), axis=-1, keepdims=True)
        y = (u - mu) * lax.rsqrt(var + EPS) * lng_ref[...] + lnb_ref[...]
        out_ref[0, r0:r0 + rc, 0:w] = (in_ref[0, r0:r0 + rc, 0:w] * acc_a).astype(BF16)
        out_ref[0, r0:r0 + rc, w:2 * w] = (y * _sigmoid(y)).astype(BF16)


def _conv_mixers(cin, lw):
    b, s, _ = cin.shape
    w = CONV_W
    const = lambda shape: pl.BlockSpec(shape, lambda i: (0,) * len(shape))
    return pl.pallas_call(
        functools.partial(_conv_kernel, min(s, 128)),
        grid=(b,),
        in_specs=[pl.BlockSpec((1, s, 3 * w), lambda i: (i, 0, 0)),
                  const((CONV_A_K, w)), const((CONV_B_K, w)), const((1, w)), const((1, w)), const((1, w))],
        out_specs=pl.BlockSpec((1, s, 2 * w), lambda i: (i, 0, 0)),
        out_shape=jax.ShapeDtypeStruct((b, s, 2 * w), BF16),
        scratch_shapes=[pltpu.VMEM((s + 2 * CONV_HALO, 2 * w), F32)],
        compiler_params=_cparams(("arbitrary",)),
        name="conv_mixers",
    )(cin, lw["conv_a_w"], lw["conv_b_w"], lw["conv_b_bias"], lw["ln_g"], lw["ln_b"])


def _attn_kernel(nkv, *refs):
    q_ref = refs[0]
    kv_refs = refs[1:1 + 2 * nkv]
    conv_ref, x_ref, g_ref, wo_ref, o_ref, att_ref = refs[1 + 2 * nkv:]
    nt_dims = (((1,), (1,)), ((), ()))
    for hd in range(HEADS):
        ls = slice(hd * HEAD_PAD, (hd + 1) * HEAD_PAD)
        qh = q_ref[0, :, ls]
        ss = [lax.dot_general(qh, kv_refs[2 * i][0, :, ls], nt_dims, preferred_element_type=F32)
              for i in range(nkv)]
        m = ss[0].max(axis=-1, keepdims=True)
        for sx in ss[1:]:
            m = jnp.maximum(m, sx.max(axis=-1, keepdims=True))
        den = None
        num = None
        for i, sx in enumerate(ss):
            p = jnp.exp((sx - m) * ATTN_SCALE)
            d_i = p.sum(axis=-1, keepdims=True)
            n_i = jnp.dot(p.astype(BF16), kv_refs[2 * i + 1][0, :, ls], preferred_element_type=F32)
            den = d_i if den is None else den + d_i
            num = n_i if num is None else num + n_i
        o = num / den
        att_ref[:, hd * V_DIM:(hd + 1) * V_DIM] = o[:, 0:V_DIM].astype(BF16)
    half = HEADS * V_DIM
    mix = jnp.dot(conv_ref[0], wo_ref[0:half, :], preferred_element_type=F32)
    mix = mix + jnp.dot(att_ref[...], wo_ref[half:2 * half, :], preferred_element_type=F32)
    o_ref[0] = x_ref[0] + g_ref[0] * mix


def _attention_block(q, kvs, conv, x, gate, w_out, tq):
    b, s, d = x.shape
    hp = HEADS * HEAD_PAD
    nkv = len(kvs) // 2
    row_spec = lambda wd: pl.BlockSpec((1, tq, wd), lambda i, j: (i, j, 0))
    kv_specs = [pl.BlockSpec((1, a.shape[1], hp), lambda i, j: (i, 0, 0)) for a in kvs]
    gate_spec = pl.BlockSpec((1, 1, d), (lambda i, j: (i, 0, 0)) if gate.shape[0] > 1 else (lambda i, j: (0, 0, 0)))
    return pl.pallas_call(
        functools.partial(_attn_kernel, nkv),
        grid=(b, s // tq),
        in_specs=[row_spec(hp)] + kv_specs + [row_spec(2 * CONV_W), row_spec(d), gate_spec,
                                              pl.BlockSpec((d, d), lambda i, j: (0, 0))],
        out_specs=row_spec(d),
        out_shape=jax.ShapeDtypeStruct((b, s, d), F32),
        scratch_shapes=[pltpu.VMEM((tq, HEADS * V_DIM), BF16)],
        compiler_params=_cparams(("arbitrary", "arbitrary")),
        name="attention_out_projection",
    )(q, *kvs, conv, x, gate, w_out)


def _top_ranks(s):
    iota = lax.broadcasted_iota(jnp.int32, s.shape, 0).astype(F32)
    rank = jnp.full(s.shape, NOT_RANKED, F32)
    vals = []
    for j in range(PEER_TOPK):
        m = jnp.max(s, axis=0, keepdims=True)
        idx = jnp.min(jnp.where(s == m, iota, float(N_KEYS)), axis=0, keepdims=True)
        sel = iota == idx
        rank = jnp.where(sel, float(j + 1), rank)
        s = jnp.where(sel, -jnp.inf, s)
        vals.append(m)
    return rank, jnp.concatenate(vals, axis=0)


def _pair_selection(a, b, ea, eb, cand_ref, cexp_ref):
    off = 0
    offs = []
    for r1 in range(PEER_TOPK):
        n = PEER_TOPK // (r1 + 1)
        cand_ref[off:off + n, :] = a[r1:r1 + 1, :] + b[0:n, :]
        cexp_ref[off:off + n, :] = ea[r1:r1 + 1, :] * eb[0:n, :]
        offs.append((off, n))
        off += n
    cand_ref[N_CAND:N_CAND_PAD, :] = jnp.full((N_CAND_PAD - N_CAND, LANES), -jnp.inf, F32)
    cexp_ref[N_CAND:N_CAND_PAD, :] = jnp.zeros((N_CAND_PAD - N_CAND, LANES), F32)
    cand = cand_ref[...]
    iota = lax.broadcasted_iota(jnp.int32, cand.shape, 0).astype(F32)
    chosen = jnp.zeros(cand.shape, F32)
    for _ in range(PEER_TOPK):
        m = jnp.max(cand, axis=0, keepdims=True)
        pos = jnp.min(jnp.where(cand == m, iota, float(N_CAND_PAD)), axis=0, keepdims=True)
        sel = iota == pos
        chosen = jnp.where(sel, 1.0, chosen)
        cand = jnp.where(sel, -jnp.inf, cand)
    z = jnp.sum(chosen * cexp_ref[...], axis=0, keepdims=True)
    cand_ref[...] = chosen
    counts = [jnp.sum(cand_ref[o:o + n, :], axis=0, keepdims=True) for (o, n) in offs]
    return counts, z


def _peer_kernel(t, ch, final, *refs):
    if final:
        (x_ref, sh_ref, sc_ref, g_ref, ng_ref, wqt_ref, sk_ref, u_ref, vt_ref, fg_ref, o_ref,
         ht_s, q_s, r2_s, e2_s, m_s, c_s, cand_s, cexp_s, act_s, w_s, acc_s) = refs
    else:
        (x_ref, sh_ref, sc_ref, g_ref, ng_ref, wqt_ref, sk_ref, u_ref, vt_ref, o_ref,
         ht_s, q_s, r2_s, e2_s, m_s, c_s, cand_s, cexp_s, act_s, w_s, acc_s) = refs
        fg_ref = None
    c = pl.program_id(1)
    n_lg = t // LANES
    i1_per_chunk = ch // N_KEYS

    @pl.when(c == 0)
    def _prologue():
        h = _rms(x_ref[...], ng_ref[...]) * (1.0 + sc_ref[0]) + sh_ref[0]
        ht_s[...] = h.T.astype(BF16)
        q_s[...] = jnp.dot(wqt_ref[...], ht_s[...], preferred_element_type=F32).astype(BF16)

        def head_body(hd, carry):
            base = pl.multiple_of(hd * (2 * PK_HALF), 2 * PK_HALF)
            s1 = jnp.dot(sk_ref[hd, 0], q_s[pl.ds(base, PK_HALF), :], preferred_element_type=F32)
            s2 = jnp.dot(sk_ref[hd, 1], q_s[pl.ds(base + PK_HALF, PK_HALF), :], preferred_element_type=F32)
            for lg in range(n_lg):
                ls = slice(lg * LANES, (lg + 1) * LANES)
                s1b = s1[:, ls]
                s2b = s2[:, ls]
                r1, a = _top_ranks(s1b)
                r2, b = _top_ranks(s2b)
                e1 = jnp.exp(s1b - a[0:1, :])
                e2 = jnp.exp(s2b - b[0:1, :])
                ea = jnp.exp(a - a[0:1, :])
                eb = jnp.exp(b - b[0:1, :])
                counts, z = _pair_selection(a, b, ea, eb, cand_s, cexp_s)
                mm = jnp.zeros(r1.shape, F32)
                for j in range(PEER_TOPK):
                    mm = jnp.where(r1 == float(j + 1), counts[j], mm)
                r2_s[hd, :, ls] = r2.astype(BF16)
                e2_s[hd, :, ls] = e2.astype(BF16)
                m_s[hd, :, ls] = mm
                c_s[hd, :, ls] = e1 / z
            return carry

        lax.fori_loop(0, PEER_HEADS, head_body, 0)

    act_s[...] = jnp.dot(u_ref[...], ht_s[...], preferred_element_type=F32)

    i1_base = pl.multiple_of(c * i1_per_chunk, 8)
    for i1l in range(i1_per_chunk):
        row0 = i1l * N_KEYS
        grp = pl.ds(i1_base + (i1l // 8) * 8, 8)
        sub = i1l % 8
        for lg in range(n_lg):
            ls = slice(lg * LANES, (lg + 1) * LANES)
            m_rows = [jnp.broadcast_to(m_s[hd, grp, ls][sub:sub + 1, :], (16, LANES)).astype(BF16)
                      for hd in range(PEER_HEADS)]
            c_rows = [jnp.broadcast_to(c_s[hd, grp, ls][sub:sub + 1, :], (16, LANES)).astype(BF16)
                      for hd in range(PEER_HEADS)]
            for r0 in range(0, N_KEYS, 16):
                gate = None
                for hd in range(PEER_HEADS):
                    r2 = r2_s[hd, r0:r0 + 16, ls]
                    e2 = e2_s[hd, r0:r0 + 16, ls]
                    term = jnp.where(r2 <= m_rows[hd], e2, jnp.zeros_like(e2)) * c_rows[hd]
                    gate = term if gate is None else gate + term
                a = act_s[row0 + r0:row0 + r0 + 16, ls]
                gelu = a * (lax.erf(a * (1.0 / math.sqrt(2.0))) + 1.0) * 0.5
                w_s[row0 + r0:row0 + r0 + 16, ls] = gelu.astype(BF16) * gate

    contrib = jnp.dot(vt_ref[...], w_s[...], preferred_element_type=F32)

    @pl.when(c == 0)
    def _first():
        acc_s[...] = contrib

    @pl.when(c > 0)
    def _rest():
        acc_s[...] = acc_s[...] + contrib

    @pl.when(c == pl.num_programs(1) - 1)
    def _epilogue():
        y = x_ref[...] + g_ref[0] * acc_s[...].T
        if final:
            y = _rms(y, fg_ref[...])
        o_ref[...] = y


def _peer_block(x2, sh, sc, gate, lw, tiles_per_mod, final_g, t, ch):
    n, d = x2.shape
    n_exp = N_KEYS * N_KEYS
    final = final_g is not None
    const = lambda shape: pl.BlockSpec(shape, lambda i, c: (0,) * len(shape))
    if sh.shape[0] > 1:
        mod_spec = pl.BlockSpec((1, 1, d), lambda i, c: (i // tiles_per_mod, 0, 0))
    else:
        mod_spec = pl.BlockSpec((1, 1, d), lambda i, c: (0, 0, 0))
    in_specs = [pl.BlockSpec((t, d), lambda i, c: (i, 0)), mod_spec, mod_spec, mod_spec, const((1, d)),
                const((PEER_HEADS * 2 * PK_HALF, d)), const((PEER_HEADS, 2, N_KEYS, PK_HALF)),
                pl.BlockSpec((ch, d), lambda i, c: (c, 0)), pl.BlockSpec((d, ch), lambda i, c: (0, c))]
    args = [x2, sh, sc, gate, lw["norm2_g"], lw["peer_wqt"], lw["peer_sk"], lw["peer_u"], lw["peer_vt"]]
    if final:
        in_specs.append(const((1, d)))
        args.append(final_g)
    tab = pltpu.VMEM((PEER_HEADS, N_KEYS, t), F32)
    tab16 = pltpu.VMEM((PEER_HEADS, N_KEYS, t), BF16)
    return pl.pallas_call(
        functools.partial(_peer_kernel, t, ch, final),
        grid=(n // t, n_exp // ch),
        in_specs=in_specs,
        out_specs=pl.BlockSpec((t, d), lambda i, c: (i, 0)),
        out_shape=jax.ShapeDtypeStruct((n, d), F32),
        scratch_shapes=[pltpu.VMEM((d, t), BF16),
                        pltpu.VMEM((PEER_HEADS * 2 * PK_HALF, t), BF16),
                        tab16, tab16, tab, tab,
                        pltpu.VMEM((N_CAND_PAD, LANES), F32), pltpu.VMEM((N_CAND_PAD, LANES), F32),
                        pltpu.VMEM((ch, t), F32),
                        pltpu.VMEM((ch, t), BF16),
                        pltpu.VMEM((d, t), F32)],
        compiler_params=_cparams(("arbitrary", "arbitrary")),
        name="peer",
    )(*args)


def _rot_cols(w):
    q = QK_ROPE // 4
    return jnp.concatenate([-w[:, q:2 * q], w[:, 0:q], -w[:, 3 * q:4 * q], w[:, 2 * q:3 * q]], axis=-1)


def _layer_weights(l, w_in, w_out, conv_a_w, conv_b_w, conv_b_bias, conv_b_ln_g, conv_b_ln_b, mla_q_norm_g,
                   w_uq, mla_kv_norm_g, w_ukv, norm2_g, peer_w_q, peer_sub_keys, peer_u, peer_v):
    d = D_MODEL
    wi = w_in[l]
    o = 5 * CONV_W + Q_LORA + KV_LORA
    w_kr = wi[:, o:o + QK_ROPE]
    zpad = lambda n: jnp.zeros((d, n), F32)
    kr_a = jnp.concatenate([zpad(QK_NOPE), w_kr, zpad(LANES - QK_NOPE - QK_ROPE)], axis=-1)
    kr_b = jnp.concatenate([zpad(QK_NOPE), _rot_cols(w_kr), zpad(LANES - QK_NOPE - QK_ROPE)], axis=-1)
    w_in_ext = jnp.concatenate([wi[:, :o], kr_a, kr_b], axis=-1).astype(BF16)

    wq3 = w_uq[l].reshape(Q_LORA, HEADS, QK_NOPE + QK_ROPE)
    zq = jnp.zeros((Q_LORA, HEADS, HEAD_PAD - QK_NOPE - QK_ROPE), F32)
    wq = jnp.concatenate([wq3, zq], axis=-1).reshape(Q_LORA, HEADS * HEAD_PAD).astype(BF16)
    rot = _rot_cols(wq3[:, :, QK_NOPE:].reshape(Q_LORA * HEADS, QK_ROPE)).reshape(Q_LORA, HEADS, QK_ROPE)
    wq_rot = jnp.concatenate([jnp.zeros((Q_LORA, HEADS, QK_NOPE), F32), rot, zq], axis=-1)
    wq_rot = wq_rot.reshape(Q_LORA, HEADS * HEAD_PAD).astype(BF16)

    wkv3 = w_ukv[l].reshape(KV_LORA, HEADS, QK_NOPE + V_DIM)
    zk = jnp.zeros((KV_LORA, HEADS, HEAD_PAD - QK_NOPE), F32)
    wk = jnp.concatenate([wkv3[:, :, :QK_NOPE], zk], axis=-1).reshape(KV_LORA, HEADS * HEAD_PAD).astype(BF16)
    zv = jnp.zeros((KV_LORA, HEADS, HEAD_PAD - V_DIM), F32)
    wv = jnp.concatenate([wkv3[:, :, QK_NOPE:], zv], axis=-1).reshape(KV_LORA, HEADS * HEAD_PAD).astype(BF16)

    return dict(
        w_in=w_in_ext, wq=wq, wq_rot=wq_rot, wk=wk, wv=wv,
        gq=mla_q_norm_g[l].reshape(1, Q_LORA), gkv=mla_kv_norm_g[l].reshape(1, KV_LORA),
        w_out=w_out[l].astype(BF16),
        conv_a_w=conv_a_w[l], conv_b_w=conv_b_w[l], conv_b_bias=conv_b_bias[l].reshape(1, CONV_W),
        ln_g=conv_b_ln_g[l].reshape(1, CONV_W), ln_b=conv_b_ln_b[l].reshape(1, CONV_W),
        norm2_g=norm2_g[l].reshape(1, d),
        peer_wqt=peer_w_q[l].T.astype(BF16),
        peer_sk=peer_sub_keys[l].astype(BF16),
        peer_u=peer_u[l].astype(BF16),
        peer_vt=peer_v[l].T.astype(BF16),
    )


def _rope_tables(n):
    rows = n // GRID_W
    row = jnp.repeat(jnp.arange(rows, dtype=F32), GRID_W)
    col = jnp.tile(jnp.arange(GRID_W, dtype=F32), rows)
    axis_dim = QK_ROPE // 2
    inv = ROPE_BASE ** (-jnp.arange(0, axis_dim, 2, dtype=F32) / axis_dim)
    ar = row[:, None] * inv[None, :]
    ac = col[:, None] * inv[None, :]
    ang = jnp.concatenate([ar, ar, ac, ac], axis=-1)
    cos, sin = jnp.cos(ang), jnp.sin(ang)
    tail = jnp.zeros((n, LANES - QK_NOPE - QK_ROPE), F32)
    cq = jnp.concatenate([jnp.ones((n, QK_NOPE), F32), cos, tail], axis=-1)
    sq = jnp.concatenate([jnp.zeros((n, QK_NOPE), F32), sin, tail], axis=-1)
    ck = jnp.concatenate([jnp.zeros((n, QK_NOPE), F32), cos, tail], axis=-1)
    return cq, sq, ck


def _identity_tables(n):
    tail = jnp.zeros((n, LANES - QK_NOPE - QK_ROPE), F32)
    ones = jnp.ones((n, QK_ROPE), F32)
    cq = jnp.concatenate([jnp.ones((n, QK_NOPE), F32), ones, tail], axis=-1)
    ck = jnp.concatenate([jnp.zeros((n, QK_NOPE), F32), ones, tail], axis=-1)
    return cq, jnp.zeros((n, LANES), F32), ck


def kernel(x, c, ctx, c_ctx, w_mod, b_mod, norm1_g, norm2_g, w_in, w_out, conv_a_w, conv_b_w, conv_b_bias,
           conv_b_ln_g, conv_b_ln_b, mla_q_norm_g, w_uq, mla_kv_norm_g, w_ukv, peer_w_q, peer_sub_keys,
           peer_u, peer_v, final_norm_g):
    b, s, d = x.shape
    n_ctx = ctx.shape[1]
    depth = w_mod.shape[0]
    ts = min(s, 512)
    ts_c = min(n_ctx, 512)
    tq = min(s, 256)
    tq_c = min(n_ctx, 256)
    t_peer = min(s, 512)
    t_peer_c = min(b * n_ctx, 512)
    ch = 1024

    rows = ((b + 1 + 7) // 8) * 8
    cvec = jnp.concatenate([c, c_ctx[None, :], jnp.zeros((rows - b - 1, d), F32)], axis=0)
    mods = _modulation(cvec, w_mod, b_mod)
    tabs_x = _rope_tables(s)
    tabs_c = _identity_tables(n_ctx)

    for l in range(depth):
        last = l == depth - 1
        lw = _layer_weights(l, w_in, w_out, conv_a_w, conv_b_w, conv_b_bias, conv_b_ln_g, conv_b_ln_b,
                            mla_q_norm_g, w_uq, mla_kv_norm_g, w_ukv, norm2_g, peer_w_q, peer_sub_keys,
                            peer_u, peer_v)
        mx = [mods[l, :b, i * d:(i + 1) * d].reshape(b, 1, d) for i in range(6)]
        mc = [mods[l, b:b + 1, i * d:(i + 1) * d].reshape(1, 1, d) for i in range(6)]
        g1 = norm1_g[l].reshape(1, d)

        conv_in_x, q_x, k_x, v_x = _in_projection(x, mx[0], mx[1], g1, lw, tabs_x, ts)
        conv_in_c, q_c, k_c, v_c = _in_projection(ctx, mc[0], mc[1], g1, lw, tabs_c, ts_c)
        conv_x = _conv_mixers(conv_in_x, lw)
        x = _attention_block(q_x, [k_x, v_x, k_c, v_c], conv_x, x, mx[2], lw["w_out"], tq)
        if not last:
            conv_c = _conv_mixers(conv_in_c, lw)
            ctx = _attention_block(q_c, [k_c, v_c], conv_c, ctx, mc[2], lw["w_out"], tq_c)

        fg = final_norm_g.reshape(1, d) if last else None
        x = _peer_block(x.reshape(b * s, d), mx[3], mx[4], mx[5], lw, s // t_peer, fg, t_peer, ch).reshape(b, s, d)
        if not last:
            ctx = _peer_block(ctx.reshape(b * n_ctx, d), mc[3], mc[4], mc[5], lw, 1, None, t_peer_c, ch)
            ctx = ctx.reshape(b, n_ctx, d)
    return x
```
